```python
import jax, jax.numpy as jnp
from jax import lax
import numpy as np

D_MODEL = 2048
BATCH = 1
SEQ = 8192
DEPTH = 4

N_MIXERS = 2
N_ATTN = (DEPTH + 1) // 2
N_REC = DEPTH // 2
HEAD_DIM = 64
N_Q_HEADS = D_MODEL // HEAD_DIM
N_KV_HEADS = 8
GROUP = N_Q_HEADS // N_KV_HEADS
WINDOW = 128
BLOCK = 128
QKV_DIM = (N_Q_HEADS + 2 * N_KV_HEADS) * HEAD_DIM
LRU_WIDTH = D_MODEL
LRU_BLOCKS = 8
LRU_BLOCK_DIM = LRU_WIDTH // LRU_BLOCKS
LRU_CONV = 4
LRU_C = 8.0
D_FF = 3 * D_MODEL
FFN_CONV = 3
EPS = 1e-6

kernel_name = "hybrid_swa_sink_rglru_convffn"


def rmsnorm(x, g):
    xf = x.astype(jnp.float32)
    y = xf * lax.rsqrt(jnp.mean(xf * xf, axis=-1, keepdims=True) + EPS)
    return (y * g.astype(jnp.float32)).astype(x.dtype)


def causal_dwconv(x, w, b):
    K = w.shape[0]
    T = x.shape[1]
    xp = jnp.pad(x, ((0, 0), (K - 1, 0), (0, 0)))
    y = b
    for k in range(K):
        y = y + xp[:, k:k + T] * w[k]
    return y


def sliding_window_attention(h, w_qkv, q_gain, k_gain, sinks, w_o):
    B, T, _ = h.shape
    nb = T // BLOCK
    qkv = h @ w_qkv
    q, k, v = jnp.split(qkv, [N_Q_HEADS * HEAD_DIM, (N_Q_HEADS + N_KV_HEADS) * HEAD_DIM], axis=-1)
    q = rmsnorm(q.reshape(B, T, N_KV_HEADS, GROUP, HEAD_DIM), q_gain)
    k = rmsnorm(k.reshape(B, T, N_KV_HEADS, HEAD_DIM), k_gain)
    v = v.reshape(B, T, N_KV_HEADS, HEAD_DIM)
    q = q.reshape(B, nb, BLOCK, N_KV_HEADS, GROUP, HEAD_DIM)
    kb = k.reshape(B, nb, BLOCK, N_KV_HEADS, HEAD_DIM)
    vb = v.reshape(B, nb, BLOCK, N_KV_HEADS, HEAD_DIM)
    pad = ((0, 0), (1, 0), (0, 0), (0, 0), (0, 0))
    kw = jnp.concatenate([jnp.pad(kb, pad)[:, :-1], kb], axis=2)
    vw = jnp.concatenate([jnp.pad(vb, pad)[:, :-1], vb], axis=2)
    scores = jnp.einsum('bnqhgd,bnkhd->bnhgqk', q, kw).astype(jnp.float32) * (HEAD_DIM ** -0.5)
    qpos = jnp.arange(BLOCK)[:, None] + BLOCK
    kpos = jnp.arange(2 * BLOCK)[None, :]
    rel = qpos - kpos
    band = (rel >= 0) & (rel < WINDOW)
    real_key = (jnp.arange(nb)[:, None] > 0) | (jnp.arange(2 * BLOCK)[None, :] >= BLOCK)
    mask = band[None, :, :] & real_key[:, None, :]
    scores = jnp.where(mask[None, :, None, None], scores, jnp.finfo(jnp.float32).min)
    sink = jnp.broadcast_to(sinks.astype(jnp.float32).reshape(1, 1, N_KV_HEADS, GROUP, 1, 1),
                            scores.shape[:-1] + (1,))
    probs = jax.nn.softmax(jnp.concatenate([scores, sink], axis=-1), axis=-1)[..., :-1]
    out = jnp.einsum('bnhgqk,bnkhd->bnqhgd', probs.astype(vw.dtype), vw)
    return out.reshape(B, T, N_Q_HEADS * HEAD_DIM) @ w_o


def rglru_block(h, w_in, conv_w, conv_b, w_a, b_a, w_i, b_i, lam, w_out):
    B, T, _ = h.shape
    xb, yb = jnp.split(h @ w_in, 2, axis=-1)
    gate = jax.nn.gelu(yb, approximate=True)
    xb = causal_dwconv(xb, conv_w, conv_b)
    xh = xb.reshape(B, T, LRU_BLOCKS, LRU_BLOCK_DIM)
    r = jax.nn.sigmoid(jnp.einsum('bthi,hij->bthj', xh, w_a) + b_a).reshape(B, T, LRU_WIDTH)
    i = jax.nn.sigmoid(jnp.einsum('bthi,hij->bthj', xh, w_i) + b_i).reshape(B, T, LRU_WIDTH)
    log_a = -LRU_C * r.astype(jnp.float32) * jax.nn.softplus(-lam.astype(jnp.float32))
    a = jnp.exp(log_a)
    u = jnp.sqrt(-jnp.expm1(2.0 * log_a)) * (i * xb).astype(jnp.float32)

    def combine(left, right):
        a1, b1 = left
        a2, b2 = right
        return a1 * a2, a2 * b1 + b2

    _, hs = lax.associative_scan(combine, (a, u), axis=1)
    return (hs.astype(h.dtype) * gate) @ w_out


def conv_ffn(h, w_up, conv_w, conv_b, w_down):
    u = causal_dwconv(h @ w_up, conv_w, conv_b)
    g, v = jnp.split(u, 2, axis=-1)
    return (jax.nn.gelu(g, approximate=True) * v) @ w_down


def setup_inputs(seed: int = 0) -> dict:
    key = jax.random.key(seed)
    ks = iter(jax.random.split(key, 32))
    f32 = jnp.float32

    def nrm(shape, scale):
        return jax.random.normal(next(ks), shape, f32) * scale

    def gain(shape):
        return 1.0 + 0.02 * jax.random.normal(next(ks), shape, f32)

    a0 = jax.random.uniform(next(ks), (N_REC, LRU_WIDTH), f32, 0.9, 0.999)
    return {
        "x": jax.random.normal(next(ks), (BATCH, SEQ, D_MODEL), f32),
        "mix_norm": gain((DEPTH, D_MODEL)),
        "ffn_norm": gain((DEPTH, D_MODEL)),
        "attn_w_qkv": nrm((N_ATTN, D_MODEL, QKV_DIM), D_MODEL ** -0.5),
        "attn_q_gain": gain((N_ATTN, HEAD_DIM)),
        "attn_k_gain": gain((N_ATTN, HEAD_DIM)),
        "attn_sinks": nrm((N_ATTN, N_Q_HEADS), 0.5),
        "attn_w_o": nrm((N_ATTN, N_Q_HEADS * HEAD_DIM, D_MODEL), (N_Q_HEADS * HEAD_DIM) ** -0.5),
        "rec_w_in": nrm((N_REC, D_MODEL, 2 * LRU_WIDTH), D_MODEL ** -0.5),
        "rec_conv_w": nrm((N_REC, LRU_CONV, LRU_WIDTH), LRU_CONV ** -0.5),
        "rec_conv_b": nrm((N_REC, LRU_WIDTH), 0.01),
        "rec_w_a": nrm((N_REC, LRU_BLOCKS, LRU_BLOCK_DIM, LRU_BLOCK_DIM), LRU_BLOCK_DIM ** -0.5),
        "rec_b_a": nrm((N_REC, LRU_BLOCKS, LRU_BLOCK_DIM), 0.01),
        "rec_w_i": nrm((N_REC, LRU_BLOCKS, LRU_BLOCK_DIM, LRU_BLOCK_DIM), LRU_BLOCK_DIM ** -0.5),
        "rec_b_i": nrm((N_REC, LRU_BLOCKS, LRU_BLOCK_DIM), 0.01),
        "rec_lambda": jnp.log(a0) - jnp.log1p(-a0),
        "rec_w_out": nrm((N_REC, LRU_WIDTH, D_MODEL), LRU_WIDTH ** -0.5),
        "ffn_w_up": nrm((DEPTH, D_MODEL, 2 * D_FF), D_MODEL ** -0.5),
        "ffn_conv_w": nrm((DEPTH, FFN_CONV, 2 * D_FF), FFN_CONV ** -0.5),
        "ffn_conv_b": nrm((DEPTH, 2 * D_FF), 0.01),
        "ffn_w_down": nrm((DEPTH, D_FF, D_MODEL), D_FF ** -0.5),
    }


def reference(x, mix_norm, ffn_norm, attn_w_qkv, attn_q_gain, attn_k_gain, attn_sinks, attn_w_o,
              rec_w_in, rec_conv_w, rec_conv_b, rec_w_a, rec_b_a, rec_w_i, rec_b_i, rec_lambda,
              rec_w_out, ffn_w_up, ffn_conv_w, ffn_conv_b, ffn_w_down):
    for layer in range(DEPTH):
        h = rmsnorm(x, mix_norm[layer])
        j = layer // N_MIXERS
        if layer % N_MIXERS == 0:
            x = x + sliding_window_attention(h, attn_w_qkv[j], attn_q_gain[j], attn_k_gain[j],
                                             attn_sinks[j], attn_w_o[j])
        else:
            x = x + rglru_block(h, rec_w_in[j], rec_conv_w[j], rec_conv_b[j], rec_w_a[j], rec_b_a[j],
                                rec_w_i[j], rec_b_i[j], rec_lambda[j], rec_w_out[j])
        h = rmsnorm(x, ffn_norm[layer])
        x = x + conv_ffn(h, ffn_w_up[layer], ffn_conv_w[layer], ffn_conv_b[layer], ffn_w_down[layer])
    return x
```

```python
import functools

import jax
import jax.numpy as jnp
from jax import lax
from jax.experimental import pallas as pl
from jax.experimental.pallas import tpu as pltpu

F32 = jnp.float32
BF16 = jnp.bfloat16

EPS = 1e-6
ATTN_BLOCK = 128
LRU_C = 8.0
LANES = 128
SUBLANES = 8
NEG_INF = float(jnp.finfo(jnp.float32).min)

VMEM_LIMIT_BYTES = 60 * 1024 * 1024


def _rmsnorm(x, g):
    ms = jnp.mean(x * x, axis=-1, keepdims=True)
    return x * lax.rsqrt(ms + EPS) * g


def _gelu_tanh(x):
    c = 0.7978845608028654
    return 0.5 * x * (1.0 + jnp.tanh(c * (x + 0.044715 * (x * x * x))))


def _one_minus_exp(z, exp_z):
    p = 1.0 / 5040.0
    for c in (1.0 / 720.0, 1.0 / 120.0, 1.0 / 24.0, 1.0 / 6.0, 0.5, 1.0):
        p = p * z + c
    return jnp.where(z > -0.35, -(p * z), 1.0 - exp_z)


def _dot(a, b):
    return jnp.dot(a, b, preferred_element_type=F32)


def _causal_taps(prev_rows, u, taps, bias):
    k = taps.shape[0]
    ext = jnp.concatenate([prev_rows, u], axis=0)
    y = bias + taps[k - 1:k, :] * ext
    for d in range(1, k):
        y = y + taps[k - 1 - d:k - d, :] * pltpu.roll(ext, d, 0)
    return y[SUBLANES:, :]


def _const_spec(shape):
    nd = len(shape)
    return pl.BlockSpec(shape, lambda *_: (0,) * nd, pipeline_mode=pl.Buffered(1))


def _ffn_body(x_ref, g_ref, wg_ref, wv_ref, cwg_ref, cwv_ref, cbg_ref, cbv_ref, wd_ref,
              o_ref, h_ref, halo_ref, *, tm):
    i = pl.program_id(0)
    j = pl.program_id(1)

    @pl.when(j == 0)
    def _():
        x = x_ref[...]
        h_ref[...] = _rmsnorm(x, g_ref[...]).astype(BF16)
        o_ref[...] = x

    @pl.when(i == 0)
    def _():
        halo_ref[:, j] = jnp.zeros((2,) + halo_ref.shape[2:], F32)

    h = h_ref[...]

    def branch(slot, w_ref, cw_ref, cb_ref):
        u = _dot(h, w_ref[...])
        y = _causal_taps(halo_ref[slot, j], u, cw_ref[...], cb_ref[...])
        halo_ref[slot, j] = u[tm - SUBLANES:, :]
        return y

    yg = branch(0, wg_ref, cwg_ref, cbg_ref)
    yv = branch(1, wv_ref, cwv_ref, cbv_ref)
    act = (_gelu_tanh(yg) * yv).astype(BF16)
    o_ref[...] += _dot(act, wd_ref[...])


def _conv_ffn(x, g, w_up, conv_w, conv_b, w_down, *, tm, tf):
    t, d = x.shape
    f = w_down.shape[0]
    assert t % tm == 0 and f % tf == 0 and tm % SUBLANES == 0 and tf % LANES == 0
    nj = f // tf
    kc = conv_w.shape[0]
    grid = (t // tm, nj)
    return pl.pallas_call(
        functools.partial(_ffn_body, tm=tm),
        grid=grid,
        in_specs=[
            pl.BlockSpec((tm, d), lambda i, j: (i, 0)),
            pl.BlockSpec((1, d), lambda i, j: (0, 0)),
            pl.BlockSpec((d, tf), lambda i, j: (0, j)),
            pl.BlockSpec((d, tf), lambda i, j: (0, j + nj)),
            pl.BlockSpec((kc, tf), lambda i, j: (0, j)),
            pl.BlockSpec((kc, tf), lambda i, j: (0, j + nj)),
            pl.BlockSpec((1, tf), lambda i, j: (0, j)),
            pl.BlockSpec((1, tf), lambda i, j: (0, j + nj)),
            pl.BlockSpec((tf, d), lambda i, j: (j, 0)),
        ],
        out_specs=pl.BlockSpec((tm, d), lambda i, j: (i, 0)),
        out_shape=jax.ShapeDtypeStruct((t, d), F32),
        scratch_shapes=[
            pltpu.VMEM((tm, d), BF16),
            pltpu.VMEM((2, nj, SUBLANES, tf), F32),
        ],
        compiler_params=pltpu.CompilerParams(
            dimension_semantics=("arbitrary", "arbitrary"),
            vmem_limit_bytes=VMEM_LIMIT_BYTES),
        name="conv_ffn",
    )(x, g.reshape(1, d), w_up, w_up, conv_w, conv_w,
      conv_b.reshape(1, -1), conv_b.reshape(1, -1), w_down)


def _head_rms_scale(v, e_sum, e_bcast, hd):
    ss = _dot((v * v).astype(BF16), e_sum)
    r = lax.rsqrt(ss * (1.0 / hd) + EPS)
    hi = r.astype(BF16)
    lo = (r - hi.astype(F32)).astype(BF16)
    return _dot(jnp.concatenate([hi, lo], axis=1), e_bcast)


def _attn_body(sinks_ref, x_ref, g_ref, wqkv_ref, wo_ref, gq_ref, gk_ref,
               esq_ref, ebq_ref, esk_ref, ebk_ref, o_ref,
               q_scr, klo_scr, khi_scr, vlo_scr, vhi_scr, ao_scr,
               *, tq, nq, nkv, hd):
    i = pl.program_id(0)
    grp = nq // nkv
    dq, dk = nq * hd, nkv * hd
    blk = ATTN_BLOCK
    nb = tq // blk
    kv_scrs = (klo_scr, khi_scr, vlo_scr, vhi_scr)

    @pl.when(i == 0)
    def _():
        for s in kv_scrs:
            s[0:blk, :] = jnp.zeros((blk, s.shape[1]), BF16)

    @pl.when(i > 0)
    def _():
        for s in kv_scrs:
            s[0:blk, :] = s[tq:tq + blk, :]

    x = x_ref[...]
    h = _rmsnorm(x, g_ref[...]).astype(BF16)
    qkv = _dot(h, wqkv_ref[...])
    q = qkv[:, :dq]
    k = qkv[:, dq:dq + dk]
    v = qkv[:, dq + dk:]
    q = q * _head_rms_scale(q, esq_ref[...], ebq_ref[...], hd) * gq_ref[...]
    k = k * _head_rms_scale(k, esk_ref[...], ebk_ref[...], hd) * gk_ref[...]
    q_scr[...] = q.astype(BF16)

    lane = lax.broadcasted_iota(jnp.int32, (tq, LANES), 1)
    is_lo = lane < hd
    ngroups = dk // LANES

    def spread(a, lo_scr, hi_scr):
        rolled = pltpu.roll(a, hd, 1)

        def group(arr, gi):
            return arr[:, gi * LANES:(gi + 1) * LANES]

        for hh in range(nkv):
            gi = hh // 2
            if hh % 2 == 0:
                lo = jnp.where(is_lo, group(a, gi), 0.0)
                hi = jnp.where(is_lo, 0.0, group(rolled, gi))
            else:
                lo = jnp.where(is_lo, group(rolled, (gi + 1) % ngroups), 0.0)
                hi = jnp.where(is_lo, 0.0, group(a, gi))
            lo_scr[blk:blk + tq, hh * LANES:(hh + 1) * LANES] = lo.astype(BF16)
            hi_scr[blk:blk + tq, hh * LANES:(hh + 1) * LANES] = hi.astype(BF16)

    spread(k, klo_scr, khi_scr)
    spread(v, vlo_scr, vhi_scr)

    qpos = lax.broadcasted_iota(jnp.int32, (blk, 2 * blk), 0) + blk
    kpos = lax.broadcasted_iota(jnp.int32, (blk, 2 * blk), 1)
    rel = qpos - kpos
    band = (rel >= 0) & (rel < blk)
    band_first = band & ((kpos >= blk) | (i > 0))

    for b in range(nb):
        mask = band_first if b == 0 else band
        rows = slice(b * blk, (b + 1) * blk)
        win = slice(b * blk, (b + 2) * blk)
        for hh in range(nkv):
            cols = slice(hh * LANES, (hh + 1) * LANES)
            for pair in range(grp // 2):
                head0 = hh * grp + 2 * pair
                qcols = slice((head0 // 2) * LANES, (head0 // 2 + 1) * LANES)
                qs = q_scr[rows, qcols]
                acc = None
                for par, (k_scr, v_scr) in enumerate(((klo_scr, vlo_scr), (khi_scr, vhi_scr))):
                    s = lax.dot_general(qs, k_scr[win, cols], (((1,), (1,)), ((), ())),
                                        preferred_element_type=F32)
                    s = jnp.where(mask, s, NEG_INF)
                    sink = sinks_ref[head0 + par]
                    m = jnp.maximum(jnp.max(s, axis=1, keepdims=True), sink)
                    p = jnp.exp(s - m)
                    denom = jnp.sum(p, axis=1, keepdims=True) + jnp.exp(sink - m)
                    pv = _dot(p.astype(BF16), v_scr[win, cols]) / denom
                    acc = pv if acc is None else acc + pv
                ao_scr[rows, qcols] = acc.astype(BF16)

    o_ref[...] = x + _dot(ao_scr[...], wo_ref[...])


def _attention_layer(x, g, w_qkv, q_gain, k_gain, sinks, w_o, *, tq):
    t, d = x.shape
    hd = q_gain.shape[0]
    nq = sinks.shape[0]
    nkv = (w_qkv.shape[1] // hd - nq) // 2
    grp = nq // nkv
    dq, dk = nq * hd, nkv * hd
    assert 2 * hd == LANES and nkv % 2 == 0 and grp % 2 == 0
    assert t % tq == 0 and tq % ATTN_BLOCK == 0 and nq <= LANES

    def head_sum(ncols):
        c = jnp.arange(ncols)[:, None] // hd
        return (c == jnp.arange(LANES)[None, :]).astype(BF16)

    def head_bcast(ncols):
        hrow = jnp.arange(2 * LANES)[:, None] % LANES
        return (hrow == jnp.arange(ncols)[None, :] // hd).astype(BF16)

    gq = (jnp.tile(q_gain, nq) * (hd ** -0.5)).reshape(1, dq)
    gk = jnp.tile(k_gain, nkv).reshape(1, dk)
    kv_rows = ATTN_BLOCK + tq
    return pl.pallas_call(
        functools.partial(_attn_body, tq=tq, nq=nq, nkv=nkv, hd=hd),
        grid=(t // tq,),
        in_specs=[
            pl.BlockSpec(memory_space=pltpu.SMEM),
            pl.BlockSpec((tq, d), lambda i: (i, 0)),
            _const_spec((1, d)),
            _const_spec(w_qkv.shape),
            _const_spec(w_o.shape),
            _const_spec((1, dq)),
            _const_spec((1, dk)),
            _const_spec((dq, LANES)),
            _const_spec((2 * LANES, dq)),
            _const_spec((dk, LANES)),
            _const_spec((2 * LANES, dk)),
        ],
        out_specs=pl.BlockSpec((tq, d), lambda i: (i, 0)),
        out_shape=jax.ShapeDtypeStruct((t, d), F32),
        scratch_shapes=[
            pltpu.VMEM((tq, dq), BF16),
            pltpu.VMEM((kv_rows, nkv * LANES), BF16),
            pltpu.VMEM((kv_rows, nkv * LANES), BF16),
            pltpu.VMEM((kv_rows, nkv * LANES), BF16),
            pltpu.VMEM((kv_rows, nkv * LANES), BF16),
            pltpu.VMEM((tq, dq), BF16),
        ],
        compiler_params=pltpu.CompilerParams(
            dimension_semantics=("arbitrary",),
            vmem_limit_bytes=VMEM_LIMIT_BYTES),
        name="swa_layer",
    )(sinks, x, g.reshape(1, d), w_qkv, w_o, gq, gk,
      head_sum(dq), head_bcast(dq), head_sum(dk), head_bcast(dk))


SCAN_LANES = 512


def _rec_body(x_ref, g_ref, win_ref, cw_ref, cb_ref, wa_ref, ba_ref, wi_ref, bi_ref,
              lam_ref, wout_ref, o_ref, halo_scr, carry_scr, a_scr, u_scr, *, tm, nblk):
    i = pl.program_id(0)
    w = a_scr.shape[1]
    bd = w // nblk

    @pl.when(i == 0)
    def _():
        halo_scr[...] = jnp.zeros(halo_scr.shape, F32)
        carry_scr[...] = jnp.zeros(carry_scr.shape, F32)

    x = x_ref[...]
    h = _rmsnorm(x, g_ref[...]).astype(BF16)
    xy = _dot(h, win_ref[...])
    xb = xy[:, :w]
    gate = _gelu_tanh(xy[:, w:])
    xc = _causal_taps(halo_scr[...], xb, cw_ref[...], cb_ref[...])
    halo_scr[...] = xb[tm - SUBLANES:, :]
    xcb = xc.astype(BF16)

    neg_lam = -lam_ref[...]
    softplus = jnp.maximum(neg_lam, 0.0) + jnp.log1p(jnp.exp(-jnp.abs(neg_lam)))
    for b in range(nblk):
        sl = slice(b * bd, (b + 1) * bd)
        r = jax.nn.sigmoid(_dot(xcb[:, sl], wa_ref[b]) + ba_ref[:, sl])
        gi = jax.nn.sigmoid(_dot(xcb[:, sl], wi_ref[b]) + bi_ref[:, sl])
        log_a = (-LRU_C) * r * softplus[:, sl]
        a = jnp.exp(log_a)
        a_scr[:, sl] = a
        u_scr[:, sl] = jnp.sqrt(_one_minus_exp(2.0 * log_a, a * a)) * (gi * xc[:, sl])

    row = lax.broadcasted_iota(jnp.int32, (SUBLANES, SCAN_LANES), 0)
    for c in range(w // SCAN_LANES):
        cs = slice(c * SCAN_LANES, (c + 1) * SCAN_LANES)

        def group(gidx, carry, cs=cs):
            r0 = pl.multiple_of(gidx * SUBLANES, SUBLANES)
            a = a_scr[pl.ds(r0, SUBLANES), cs]
            u = u_scr[pl.ds(r0, SUBLANES), cs]
            for s in (1, 2, 4):
                keep = row >= s
                u = jnp.where(keep, a * pltpu.roll(u, s, 0) + u, u)
                a = jnp.where(keep, a * pltpu.roll(a, s, 0), a)
            hs = a * carry + u
            u_scr[pl.ds(r0, SUBLANES), cs] = hs
            return hs[SUBLANES - 1:SUBLANES, :]

        carry_scr[0:1, cs] = lax.fori_loop(0, tm // SUBLANES, group, carry_scr[0:1, cs],
                                           unroll=2)

    y = (u_scr[...] * gate).astype(BF16)
    o_ref[...] = x + _dot(y, wout_ref[...])


def _recurrent_layer(x, g, w_in, conv_w, conv_b, w_a, b_a, w_i, b_i, lam, w_out, *, tm):
    t, d = x.shape
    w = w_out.shape[0]
    nblk = w_a.shape[0]
    kc = conv_w.shape[0]
    assert t % tm == 0 and tm % SUBLANES == 0 and w % SCAN_LANES == 0
    return pl.pallas_call(
        functools.partial(_rec_body, tm=tm, nblk=nblk),
        grid=(t // tm,),
        in_specs=[
            pl.BlockSpec((tm, d), lambda i: (i, 0)),
            _const_spec((1, d)),
            _const_spec(w_in.shape),
            _const_spec((kc, w)),
            _const_spec((1, w)),
            _const_spec(w_a.shape),
            _const_spec((1, w)),
            _const_spec(w_i.shape),
            _const_spec((1, w)),
            _const_spec((1, w)),
            _const_spec(w_out.shape),
        ],
        out_specs=pl.BlockSpec((tm, d), lambda i: (i, 0)),
        out_shape=jax.ShapeDtypeStruct((t, d), F32),
        scratch_shapes=[
            pltpu.VMEM((SUBLANES, w), F32),
            pltpu.VMEM((SUBLANES, w), F32),
            pltpu.VMEM((tm, w), F32),
            pltpu.VMEM((tm, w), F32),
        ],
        compiler_params=pltpu.CompilerParams(
            dimension_semantics=("arbitrary",),
            vmem_limit_bytes=VMEM_LIMIT_BYTES),
        name="rglru_layer",
    )(x, g.reshape(1, d), w_in, conv_w, conv_b.reshape(1, w), w_a, b_a.reshape(1, w),
      w_i, b_i.reshape(1, w), lam.reshape(1, w), w_out)


ATTN_TQ = 256
REC_TM = 256
FFN_TM = 512
FFN_TF = 512


def kernel(x, mix_norm, ffn_norm, attn_w_qkv, attn_q_gain, attn_k_gain, attn_sinks, attn_w_o,
           rec_w_in, rec_conv_w, rec_conv_b, rec_w_a, rec_b_a, rec_w_i, rec_b_i, rec_lambda,
           rec_w_out, ffn_w_up, ffn_conv_w, ffn_conv_b, ffn_w_down):
    b, t, d = x.shape
    depth = mix_norm.shape[0]
    n_mixers = 2
    bf = lambda a: a.astype(BF16)
    outs = []
    for bi in range(b):
        xs = x[bi]
        for layer in range(depth):
            j = layer // n_mixers
            if layer % n_mixers == 0:
                xs = _attention_layer(xs, mix_norm[layer], bf(attn_w_qkv[j]), attn_q_gain[j],
                                      attn_k_gain[j], attn_sinks[j], bf(attn_w_o[j]),
                                      tq=min(ATTN_TQ, t))
            else:
                xs = _recurrent_layer(xs, mix_norm[layer], bf(rec_w_in[j]), rec_conv_w[j],
                                      rec_conv_b[j], bf(rec_w_a[j]), rec_b_a[j], bf(rec_w_i[j]),
                                      rec_b_i[j], rec_lambda[j], bf(rec_w_out[j]),
                                      tm=min(REC_TM, t))
            xs = _conv_ffn(xs, ffn_norm[layer], bf(ffn_w_up[layer]), ffn_conv_w[layer],
                           ffn_conv_b[layer], bf(ffn_w_down[layer]),
                           tm=min(FFN_TM, t), tf=min(FFN_TF, ffn_w_down.shape[1]))
        outs.append(xs)
    return jnp.stack(outs, axis=0)
```

```python
import functools

import jax
import jax.numpy as jnp
from jax import lax
from jax.experimental import pallas as pl
from jax.experimental.pallas import tpu as pltpu

F32 = jnp.float32
BF16 = jnp.bfloat16

EPS = 1e-6
ATTN_BLOCK = 128
LRU_C = 8.0
LANES = 128
SUBLANES = 8
NEG_INF = float(jnp.finfo(jnp.float32).min)

VMEM_LIMIT_BYTES = 60 * 1024 * 1024


def _rmsnorm(x, g):
    ms = jnp.mean(x * x, axis=-1, keepdims=True)
    return x * lax.rsqrt(ms + EPS) * g


def _gelu_tanh(x):
    c = 0.7978845608028654
    return 0.5 * x * (1.0 + jnp.tanh(c * (x + 0.044715 * (x * x * x))))


def _one_minus_exp(z, exp_z):
    p = 1.0 / 5040.0
    for c in (1.0 / 720.0, 1.0 / 120.0, 1.0 / 24.0, 1.0 / 6.0, 0.5, 1.0):
        p = p * z + c
    return jnp.where(z > -0.35, -(p * z), 1.0 - exp_z)


def _dot(a, b):
    return jnp.dot(a, b, preferred_element_type=F32)


def _causal_taps(prev_rows, u, taps, bias):
    k = taps.shape[0]
    ext = jnp.concatenate([prev_rows, u], axis=0)
    y = bias + taps[k - 1:k, :] * ext
    for d in range(1, k):
        y = y + taps[k - 1 - d:k - d, :] * pltpu.roll(ext, d, 0)
    return y[SUBLANES:, :]


def _const_spec(shape):
    nd = len(shape)
    return pl.BlockSpec(shape, lambda *_: (0,) * nd, pipeline_mode=pl.Buffered(1))


def _layer_spec(stacked, layer):
    nd = stacked.ndim - 1
    return pl.BlockSpec((None,) + stacked.shape[1:], lambda *_: (layer,) + (0,) * nd,
                        pipeline_mode=pl.Buffered(1))


def _rows(p):
    return p.reshape(p.shape[0], 1, p.shape[1])


def _ffn_body(x_ref, g_ref, wg_ref, wv_ref, cwg_ref, cwv_ref, cbg_ref, cbv_ref, wd_ref,
              o_ref, h_ref, halo_ref, *, tm):
    i = pl.program_id(0)
    j = pl.program_id(1)

    @pl.when(j == 0)
    def _():
        x = x_ref[...]
        h_ref[...] = _rmsnorm(x, g_ref[...]).astype(BF16)
        o_ref[...] = x

    @pl.when(i == 0)
    def _():
        halo_ref[:, j] = jnp.zeros((2,) + halo_ref.shape[2:], F32)

    h = h_ref[...]

    def branch(slot, w_ref, cw_ref, cb_ref):
        u = _dot(h, w_ref[...])
        y = _causal_taps(halo_ref[slot, j], u, cw_ref[...], cb_ref[...])
        halo_ref[slot, j] = u[tm - SUBLANES:, :]
        return y

    yg = branch(0, wg_ref, cwg_ref, cbg_ref)
    yv = branch(1, wv_ref, cwv_ref, cbv_ref)
    act = (_gelu_tanh(yg) * yv).astype(BF16)
    o_ref[...] += _dot(act, wd_ref[...])


def _conv_ffn(x, layer, g, w_up, conv_w, conv_b, w_down, *, tm, tf):
    t, d = x.shape
    f = w_down.shape[1]
    assert t % tm == 0 and f % tf == 0 and tm % SUBLANES == 0 and tf % LANES == 0
    nj = f // tf
    kc = conv_w.shape[1]
    grid = (t // tm, nj)
    g, conv_b = _rows(g), _rows(conv_b)
    return pl.pallas_call(
        functools.partial(_ffn_body, tm=tm),
        grid=grid,
        in_specs=[
            pl.BlockSpec((tm, d), lambda i, j: (i, 0)),
            pl.BlockSpec((None, 1, d), lambda i, j: (layer, 0, 0)),
            pl.BlockSpec((None, d, tf), lambda i, j: (layer, 0, j)),
            pl.BlockSpec((None, d, tf), lambda i, j: (layer, 0, j + nj)),
            pl.BlockSpec((None, kc, tf), lambda i, j: (layer, 0, j)),
            pl.BlockSpec((None, kc, tf), lambda i, j: (layer, 0, j + nj)),
            pl.BlockSpec((None, 1, tf), lambda i, j: (layer, 0, j)),
            pl.BlockSpec((None, 1, tf), lambda i, j: (layer, 0, j + nj)),
            pl.BlockSpec((None, tf, d), lambda i, j: (layer, j, 0)),
        ],
        out_specs=pl.BlockSpec((tm, d), lambda i, j: (i, 0)),
        out_shape=jax.ShapeDtypeStruct((t, d), F32),
        scratch_shapes=[
            pltpu.VMEM((tm, d), BF16),
            pltpu.VMEM((2, nj, SUBLANES, tf), F32),
        ],
        compiler_params=pltpu.CompilerParams(
            dimension_semantics=("arbitrary", "arbitrary"),
            vmem_limit_bytes=VMEM_LIMIT_BYTES),
        name="conv_ffn",
    )(x, g, w_up, w_up, conv_w, conv_w, conv_b, conv_b, w_down)


def _head_rms_scale(v, e_sum, e_bcast, hd):
    ss = _dot((v * v).astype(BF16), e_sum)
    r = lax.rsqrt(ss * (1.0 / hd) + EPS)
    hi = r.astype(BF16)
    lo = (r - hi.astype(F32)).astype(BF16)
    return _dot(jnp.concatenate([hi, lo], axis=1), e_bcast)


def _attn_body(sinks_ref, x_ref, g_ref, wqkv_ref, wo_ref, gq_ref, gk_ref,
               esq_ref, ebq_ref, esk_ref, ebk_ref, o_ref,
               q_scr, klo_scr, khi_scr, vlo_scr, vhi_scr, ao_scr,
               *, tq, nq, nkv, hd):
    i = pl.program_id(0)
    grp = nq // nkv
    dq, dk = nq * hd, nkv * hd
    blk = ATTN_BLOCK
    nb = tq // blk
    kv_scrs = (klo_scr, khi_scr, vlo_scr, vhi_scr)

    @pl.when(i == 0)
    def _():
        for s in kv_scrs:
            s[0:blk, :] = jnp.zeros((blk, s.shape[1]), BF16)

    @pl.when(i > 0)
    def _():
        for s in kv_scrs:
            s[0:blk, :] = s[tq:tq + blk, :]

    x = x_ref[...]
    h = _rmsnorm(x, g_ref[...]).astype(BF16)
    qkv = _dot(h, wqkv_ref[...])
    q = qkv[:, :dq]
    k = qkv[:, dq:dq + dk]
    v = qkv[:, dq + dk:]
    q = q * _head_rms_scale(q, esq_ref[...], ebq_ref[...], hd) * gq_ref[...]
    k = k * _head_rms_scale(k, esk_ref[...], ebk_ref[...], hd) * gk_ref[...]
    q_scr[...] = q.astype(BF16)

    lane = lax.broadcasted_iota(jnp.int32, (tq, LANES), 1)
    is_lo = lane < hd
    ngroups = dk // LANES

    def spread(a, lo_scr, hi_scr):
        rolled = pltpu.roll(a, hd, 1)

        def group(arr, gi):
            return arr[:, gi * LANES:(gi + 1) * LANES]

        for hh in range(nkv):
            gi = hh // 2
            if hh % 2 == 0:
                lo = jnp.where(is_lo, group(a, gi), 0.0)
                hi = jnp.where(is_lo, 0.0, group(rolled, gi))
            else:
                lo = jnp.where(is_lo, group(rolled, (gi + 1) % ngroups), 0.0)
                hi = jnp.where(is_lo, 0.0, group(a, gi))
            lo_scr[blk:blk + tq, hh * LANES:(hh + 1) * LANES] = lo.astype(BF16)
            hi_scr[blk:blk + tq, hh * LANES:(hh + 1) * LANES] = hi.astype(BF16)

    spread(k, klo_scr, khi_scr)
    spread(v, vlo_scr, vhi_scr)

    qpos = lax.broadcasted_iota(jnp.int32, (blk, 2 * blk), 0) + blk
    kpos = lax.broadcasted_iota(jnp.int32, (blk, 2 * blk), 1)
    rel = qpos - kpos
    band = (rel >= 0) & (rel < blk)
    band_first = band & ((kpos >= blk) | (i > 0))

    for b in range(nb):
        mask = band_first if b == 0 else band
        rows = slice(b * blk, (b + 1) * blk)
        win = slice(b * blk, (b + 2) * blk)
        for hh in range(nkv):
            cols = slice(hh * LANES, (hh + 1) * LANES)
            for pair in range(grp // 2):
                head0 = hh * grp + 2 * pair
                qcols = slice((head0 // 2) * LANES, (head0 // 2 + 1) * LANES)
                qs = q_scr[rows, qcols]
                acc = None
                for par, (k_scr, v_scr) in enumerate(((klo_scr, vlo_scr), (khi_scr, vhi_scr))):
                    s = lax.dot_general(qs, k_scr[win, cols], (((1,), (1,)), ((), ())),
                                        preferred_element_type=F32)
                    s = jnp.where(mask, s, NEG_INF)
                    sink = sinks_ref[head0 + par]
                    m = jnp.maximum(jnp.max(s, axis=1, keepdims=True), sink)
                    p = jnp.exp(s - m)
                    denom = jnp.sum(p, axis=1, keepdims=True) + jnp.exp(sink - m)
                    pv = _dot(p.astype(BF16), v_scr[win, cols]) / denom
                    acc = pv if acc is None else acc + pv
                ao_scr[rows, qcols] = acc.astype(BF16)

    o_ref[...] = x + _dot(ao_scr[...], wo_ref[...])


def _attention_layer(x, layer, j, g, w_qkv, q_gain, k_gain, sinks, w_o, *, tq):
    t, d = x.shape
    hd = q_gain.shape[1]
    nq = sinks.shape[1]
    nkv = (w_qkv.shape[2] // hd - nq) // 2
    grp = nq // nkv
    dq, dk = nq * hd, nkv * hd
    assert 2 * hd == LANES and nkv % 2 == 0 and grp % 2 == 0
    assert t % tq == 0 and tq % ATTN_BLOCK == 0 and nq <= LANES

    def head_sum(ncols):
        c = jnp.arange(ncols)[:, None] // hd
        return (c == jnp.arange(LANES)[None, :]).astype(BF16)

    def head_bcast(ncols):
        hrow = jnp.arange(2 * LANES)[:, None] % LANES
        return (hrow == jnp.arange(ncols)[None, :] // hd).astype(BF16)

    gq = (jnp.tile(q_gain[j], nq) * (hd ** -0.5)).reshape(1, dq)
    gk = jnp.tile(k_gain[j], nkv).reshape(1, dk)
    kv_rows = ATTN_BLOCK + tq
    return pl.pallas_call(
        functools.partial(_attn_body, tq=tq, nq=nq, nkv=nkv, hd=hd),
        grid=(t // tq,),
        in_specs=[
            pl.BlockSpec(memory_space=pltpu.SMEM),
            pl.BlockSpec((tq, d), lambda i: (i, 0)),
            _layer_spec(_rows(g), layer),
            _layer_spec(w_qkv, j),
            _layer_spec(w_o, j),
            _const_spec((1, dq)),
            _const_spec((1, dk)),
            _const_spec((dq, LANES)),
            _const_spec((2 * LANES, dq)),
            _const_spec((dk, LANES)),
            _const_spec((2 * LANES, dk)),
        ],
        out_specs=pl.BlockSpec((tq, d), lambda i: (i, 0)),
        out_shape=jax.ShapeDtypeStruct((t, d), F32),
        scratch_shapes=[
            pltpu.VMEM((tq, dq), BF16),
            pltpu.VMEM((kv_rows, nkv * LANES), BF16),
            pltpu.VMEM((kv_rows, nkv * LANES), BF16),
            pltpu.VMEM((kv_rows, nkv * LANES), BF16),
            pltpu.VMEM((kv_rows, nkv * LANES), BF16),
            pltpu.VMEM((tq, dq), BF16),
        ],
        compiler_params=pltpu.CompilerParams(
            dimension_semantics=("arbitrary",),
            vmem_limit_bytes=VMEM_LIMIT_BYTES),
        name="swa_layer",
    )(sinks[j], x, _rows(g), w_qkv, w_o, gq, gk,
      head_sum(dq), head_bcast(dq), head_sum(dk), head_bcast(dk))


SCAN_LANES = 512


def _rec_body(x_ref, g_ref, win_ref, cw_ref, cb_ref, wa_ref, ba_ref, wi_ref, bi_ref,
              lam_ref, wout_ref, o_ref, halo_scr, carry_scr, a_scr, u_scr, *, tm, nblk):
    i = pl.program_id(0)
    w = a_scr.shape[1]
    bd = w // nblk

    @pl.when(i == 0)
    def _():
        halo_scr[...] = jnp.zeros(halo_scr.shape, F32)
        carry_scr[...] = jnp.zeros(carry_scr.shape, F32)

    x = x_ref[...]
    h = _rmsnorm(x, g_ref[...]).astype(BF16)
    xy = _dot(h, win_ref[...])
    xb = xy[:, :w]
    gate = _gelu_tanh(xy[:, w:])
    xc = _causal_taps(halo_scr[...], xb, cw_ref[...], cb_ref[...])
    halo_scr[...] = xb[tm - SUBLANES:, :]
    xcb = xc.astype(BF16)

    neg_lam = -lam_ref[...]
    softplus = jnp.maximum(neg_lam, 0.0) + jnp.log1p(jnp.exp(-jnp.abs(neg_lam)))
    for b in range(nblk):
        sl = slice(b * bd, (b + 1) * bd)
        r = jax.nn.sigmoid(_dot(xcb[:, sl], wa_ref[b]) + ba_ref[:, sl])
        gi = jax.nn.sigmoid(_dot(xcb[:, sl], wi_ref[b]) + bi_ref[:, sl])
        log_a = (-LRU_C) * r * softplus[:, sl]
        a = jnp.exp(log_a)
        a_scr[:, sl] = a
        u_scr[:, sl] = jnp.sqrt(_one_minus_exp(2.0 * log_a, a * a)) * (gi * xc[:, sl])

    row = lax.broadcasted_iota(jnp.int32, (SUBLANES, SCAN_LANES), 0)
    for c in range(w // SCAN_LANES):
        cs = slice(c * SCAN_LANES, (c + 1) * SCAN_LANES)

        def group(gidx, carry, cs=cs):
            r0 = pl.multiple_of(gidx * SUBLANES, SUBLANES)
            a = a_scr[pl.ds(r0, SUBLANES), cs]
            u = u_scr[pl.ds(r0, SUBLANES), cs]
            for s in (1, 2, 4):
                keep = row >= s
                u = jnp.where(keep, a * pltpu.roll(u, s, 0) + u, u)
                a = jnp.where(keep, a * pltpu.roll(a, s, 0), a)
            hs = a * carry + u
            u_scr[pl.ds(r0, SUBLANES), cs] = hs
            return hs[SUBLANES - 1:SUBLANES, :]

        carry_scr[0:1, cs] = lax.fori_loop(0, tm // SUBLANES, group, carry_scr[0:1, cs],
                                           unroll=2)

    y = (u_scr[...] * gate).astype(BF16)
    o_ref[...] = x + _dot(y, wout_ref[...])


def _recurrent_layer(x, layer, j, g, w_in, conv_w, conv_b, w_a, b_a, w_i, b_i, lam, w_out, *, tm):
    t, d = x.shape
    w = w_out.shape[1]
    nblk = w_a.shape[1]
    assert t % tm == 0 and tm % SUBLANES == 0 and w % SCAN_LANES == 0
    g, conv_b, lam = _rows(g), _rows(conv_b), _rows(lam)
    b_a = b_a.reshape(b_a.shape[0], 1, w)
    b_i = b_i.reshape(b_i.shape[0], 1, w)
    return pl.pallas_call(
        functools.partial(_rec_body, tm=tm, nblk=nblk),
        grid=(t // tm,),
        in_specs=[
            pl.BlockSpec((tm, d), lambda i: (i, 0)),
            _layer_spec(g, layer),
            _layer_spec(w_in, j),
            _layer_spec(conv_w, j),
            _layer_spec(conv_b, j),
            _layer_spec(w_a, j),
            _layer_spec(b_a, j),
            _layer_spec(w_i, j),
            _layer_spec(b_i, j),
            _layer_spec(lam, j),
            _layer_spec(w_out, j),
        ],
        out_specs=pl.BlockSpec((tm, d), lambda i: (i, 0)),
        out_shape=jax.ShapeDtypeStruct((t, d), F32),
        scratch_shapes=[
            pltpu.VMEM((SUBLANES, w), F32),
            pltpu.VMEM((SUBLANES, w), F32),
            pltpu.VMEM((tm, w), F32),
            pltpu.VMEM((tm, w), F32),
        ],
        compiler_params=pltpu.CompilerParams(
            dimension_semantics=("arbitrary",),
            vmem_limit_bytes=VMEM_LIMIT_BYTES),
        name="rglru_layer",
    )(x, g, w_in, conv_w, conv_b, w_a, b_a, w_i, b_i, lam, w_out)


ATTN_TQ = 256
REC_TM = 256
FFN_TM = 1024
FFN_TF = 512


def kernel(x, mix_norm, ffn_norm, attn_w_qkv, attn_q_gain, attn_k_gain, attn_sinks, attn_w_o,
           rec_w_in, rec_conv_w, rec_conv_b, rec_w_a, rec_b_a, rec_w_i, rec_b_i, rec_lambda,
           rec_w_out, ffn_w_up, ffn_conv_w, ffn_conv_b, ffn_w_down):
    b, t, d = x.shape
    depth = mix_norm.shape[0]
    n_mixers = 2
    attn_w_qkv, attn_w_o, rec_w_in, rec_w_a, rec_w_i, rec_w_out, ffn_w_up, ffn_w_down = (
        a.astype(BF16) for a in (attn_w_qkv, attn_w_o, rec_w_in, rec_w_a, rec_w_i, rec_w_out,
                                 ffn_w_up, ffn_w_down))
    outs = []
    for bi in range(b):
        xs = x[bi]
        for layer in range(depth):
            j = layer // n_mixers
            if layer % n_mixers == 0:
                xs = _attention_layer(xs, layer, j, mix_norm, attn_w_qkv, attn_q_gain, attn_k_gain,
                                      attn_sinks, attn_w_o, tq=min(ATTN_TQ, t))
            else:
                xs = _recurrent_layer(xs, layer, j, mix_norm, rec_w_in, rec_conv_w, rec_conv_b,
                                      rec_w_a, rec_b_a, rec_w_i, rec_b_i, rec_lambda, rec_w_out,
                                      tm=min(REC_TM, t))
            xs = _conv_ffn(xs, layer, ffn_norm, ffn_w_up, ffn_conv_w, ffn_conv_b, ffn_w_down,
                           tm=min(FFN_TM, t), tf=min(FFN_TF, ffn_w_down.shape[1]))
        outs.append(xs)
    return jnp.stack(outs, axis=0)
```

```python
import functools

import jax
import jax.numpy as jnp
from jax import lax
from jax.experimental import pallas as pl
from jax.experimental.pallas import tpu as pltpu

F32 = jnp.float32
BF16 = jnp.bfloat16

EPS = 1e-6
ATTN_BLOCK = 128
LRU_C = 8.0
LANES = 128
SUBLANES = 8
NEG_INF = float(jnp.finfo(jnp.float32).min)

VMEM_LIMIT_BYTES = 60 * 1024 * 1024


def _rmsnorm(x, g):
    ms = jnp.mean(x * x, axis=-1, keepdims=True)
    return x * lax.rsqrt(ms + EPS) * g


def _gelu_tanh(x):
    c = 2.0 * 0.7978845608028654
    return x * jax.nn.sigmoid(x * (c + (c * 0.044715) * (x * x)))


def _one_minus_exp2(x):
    t = jnp.tanh(x)
    return (-2.0 * t) / (1.0 - t)


def _dot(a, b):
    return jnp.dot(a, b, preferred_element_type=F32)


def _causal_taps(prev_rows, u, taps, bias):
    k = taps.shape[0]
    ext = jnp.concatenate([prev_rows, u], axis=0)
    y = bias + taps[k - 1:k, :] * ext
    for d in range(1, k):
        y = y + taps[k - 1 - d:k - d, :] * pltpu.roll(ext, d, 0)
    return y[SUBLANES:, :]


def _causal_taps_ref(ext_ref, cols, u, taps, bias):
    rows = u.shape[0]
    k = taps.shape[0]
    ext_ref[SUBLANES:, cols] = u
    y = bias + taps[k - 1:k, :] * u
    for d in range(1, k):
        y = y + taps[k - 1 - d:k - d, :] * ext_ref[pl.ds(SUBLANES - d, rows), cols]
    ext_ref[0:SUBLANES, cols] = u[rows - SUBLANES:, :]
    return y


def _const_spec(shape):
    nd = len(shape)
    return pl.BlockSpec(shape, lambda *_: (0,) * nd, pipeline_mode=pl.Buffered(1))


def _layer_spec(stacked, layer):
    nd = stacked.ndim - 1
    return pl.BlockSpec((None,) + stacked.shape[1:], lambda *_: (layer,) + (0,) * nd,
                        pipeline_mode=pl.Buffered(1))


def _rows(p):
    return p.reshape(p.shape[0], 1, p.shape[1])


def _ffn_body(*refs, tm, n_convert):
    (x_ref, g_ref, wg_ref, wv_ref, cwg_ref, cwv_ref, cbg_ref, cbv_ref, wd_ref), rest = refs[:9], refs[9:]
    src_refs, o_ref, dst_refs = rest[:n_convert], rest[n_convert], rest[n_convert + 1:2 * n_convert + 1]
    h_ref, halo_ref = rest[2 * n_convert + 1:]
    i = pl.program_id(0)
    j = pl.program_id(1)

    @pl.when(j == 0)
    def _():
        x = x_ref[...]
        h_ref[...] = _rmsnorm(x, g_ref[...]).astype(BF16)
        o_ref[...] = x

    @pl.when(i == 0)
    def _():
        halo_ref[:, j] = jnp.zeros((2,) + halo_ref.shape[2:], F32)

    for src_ref, dst_ref in zip(src_refs, dst_refs):
        dst_ref[...] = src_ref[...].astype(BF16)

    h = h_ref[...]

    def branch(slot, w_ref, cw_ref, cb_ref):
        u = _dot(h, w_ref[...])
        y = _causal_taps(halo_ref[slot, j], u, cw_ref[...], cb_ref[...])
        halo_ref[slot, j] = u[tm - SUBLANES:, :]
        return y

    yg = branch(0, wg_ref, cwg_ref, cbg_ref)
    yv = branch(1, wv_ref, cwv_ref, cbv_ref)
    act = (_gelu_tanh(yg) * yv).astype(BF16)
    o_ref[...] += _dot(act, wd_ref[...])


def _chunk_specs(stacked, layer, nsteps, step_of):
    _, r, c = stacked.shape
    if c % (nsteps * LANES) == 0:
        blk = (r, c // nsteps)
        idx = lambda *ids: (0, step_of(*ids))
    else:
        assert r % (nsteps * 2 * SUBLANES) == 0
        blk = (r // nsteps, c)
        idx = lambda *ids: (step_of(*ids), 0)
    in_spec = pl.BlockSpec((None,) + blk, lambda *ids: (layer,) + idx(*ids))
    return in_spec, pl.BlockSpec(blk, idx), jax.ShapeDtypeStruct((r, c), BF16)


def _conv_ffn(x, layer, g, w_up, conv_w, conv_b, w_down, convert=(), *, tm, tf):
    t, d = x.shape
    f = w_down.shape[0]
    assert t % tm == 0 and f % tf == 0 and tm % SUBLANES == 0 and tf % LANES == 0
    nj = f // tf
    kc = conv_w.shape[1]
    grid = (t // tm, nj)
    g, conv_b = _rows(g), _rows(conv_b)
    chunks = [_chunk_specs(a, l, grid[0] * nj, lambda i, j: i * nj + j) for a, l in convert]
    outs = pl.pallas_call(
        functools.partial(_ffn_body, tm=tm, n_convert=len(convert)),
        grid=grid,
        in_specs=[
            pl.BlockSpec((tm, d), lambda i, j: (i, 0)),
            pl.BlockSpec((None, 1, d), lambda i, j: (layer, 0, 0)),
            pl.BlockSpec((d, tf), lambda i, j: (0, j)),
            pl.BlockSpec((d, tf), lambda i, j: (0, j + nj)),
            pl.BlockSpec((None, kc, tf), lambda i, j: (layer, 0, j)),
            pl.BlockSpec((None, kc, tf), lambda i, j: (layer, 0, j + nj)),
            pl.BlockSpec((None, 1, tf), lambda i, j: (layer, 0, j)),
            pl.BlockSpec((None, 1, tf), lambda i, j: (layer, 0, j + nj)),
            pl.BlockSpec((tf, d), lambda i, j: (j, 0)),
        ] + [c[0] for c in chunks],
        out_specs=[pl.BlockSpec((tm, d), lambda i, j: (i, 0))] + [c[1] for c in chunks],
        out_shape=[jax.ShapeDtypeStruct((t, d), F32)] + [c[2] for c in chunks],
        scratch_shapes=[
            pltpu.VMEM((tm, d), BF16),
            pltpu.VMEM((2, nj, SUBLANES, tf), F32),
        ],
        compiler_params=pltpu.CompilerParams(
            dimension_semantics=("arbitrary", "arbitrary"),
            vmem_limit_bytes=VMEM_LIMIT_BYTES),
        name="conv_ffn",
    )(x, g, w_up, w_up, conv_w, conv_w, conv_b, conv_b, w_down, *[a for a, _ in convert])
    return outs[0], outs[1:]


def _head_rms_scale(v, e_sum, e_bcast, hd):
    ss = _dot((v * v).astype(BF16), e_sum)
    r = lax.rsqrt(ss * (1.0 / hd) + EPS)
    hi = r.astype(BF16)
    lo = (r - hi.astype(F32)).astype(BF16)
    return _dot(jnp.concatenate([hi, lo], axis=1), e_bcast)


def _attn_body(sinks_ref, x_ref, g_ref, wqkv_ref, wo_ref, gq_ref, gk_ref,
               esq_ref, ebq_ref, esk_ref, ebk_ref, o_ref,
               q_scr, klo_scr, khi_scr, vlo_scr, vhi_scr, ao_scr,
               *, tq, nq, nkv, hd):
    i = pl.program_id(0)
    grp = nq // nkv
    dq, dk = nq * hd, nkv * hd
    blk = ATTN_BLOCK
    nb = tq // blk
    kv_scrs = (klo_scr, khi_scr, vlo_scr, vhi_scr)

    @pl.when(i == 0)
    def _():
        for s in kv_scrs:
            s[0:blk, :] = jnp.zeros((blk, s.shape[1]), BF16)

    @pl.when(i > 0)
    def _():
        for s in kv_scrs:
            s[0:blk, :] = s[tq:tq + blk, :]

    x = x_ref[...]
    h = _rmsnorm(x, g_ref[...]).astype(BF16)
    qkv = _dot(h, wqkv_ref[...])
    q = qkv[:, :dq]
    k = qkv[:, dq:dq + dk]
    v = qkv[:, dq + dk:]
    q = q * _head_rms_scale(q, esq_ref[...], ebq_ref[...], hd) * gq_ref[...]
    k = k * _head_rms_scale(k, esk_ref[...], ebk_ref[...], hd) * gk_ref[...]
    q_scr[...] = q.astype(BF16)

    lane = lax.broadcasted_iota(jnp.int32, (tq, LANES), 1)
    is_lo = lane < hd
    ngroups = dk // LANES

    def spread(a, lo_scr, hi_scr):
        rolled = pltpu.roll(a, hd, 1)

        def group(arr, gi):
            return arr[:, gi * LANES:(gi + 1) * LANES]

        for hh in range(nkv):
            gi = hh // 2
            if hh % 2 == 0:
                lo = jnp.where(is_lo, group(a, gi), 0.0)
                hi = jnp.where(is_lo, 0.0, group(rolled, gi))
            else:
                lo = jnp.where(is_lo, group(rolled, (gi + 1) % ngroups), 0.0)
                hi = jnp.where(is_lo, 0.0, group(a, gi))
            lo_scr[blk:blk + tq, hh * LANES:(hh + 1) * LANES] = lo.astype(BF16)
            hi_scr[blk:blk + tq, hh * LANES:(hh + 1) * LANES] = hi.astype(BF16)

    spread(k, klo_scr, khi_scr)
    spread(v, vlo_scr, vhi_scr)

    qpos = lax.broadcasted_iota(jnp.int32, (blk, 2 * blk), 0) + blk
    kpos = lax.broadcasted_iota(jnp.int32, (blk, 2 * blk), 1)
    rel = qpos - kpos
    band = (rel >= 0) & (rel < blk)
    band_first = band & ((kpos >= blk) | (i > 0))

    for b in range(nb):
        mask = band_first if b == 0 else band
        rows = slice(b * blk, (b + 1) * blk)
        win = slice(b * blk, (b + 2) * blk)
        for hh in range(nkv):
            cols = slice(hh * LANES, (hh + 1) * LANES)
            for pair in range(grp // 2):
                head0 = hh * grp + 2 * pair
                qcols = slice((head0 // 2) * LANES, (head0 // 2 + 1) * LANES)
                qs = q_scr[rows, qcols]
                acc = None
                for par, (k_scr, v_scr) in enumerate(((klo_scr, vlo_scr), (khi_scr, vhi_scr))):
                    s = lax.dot_general(qs, k_scr[win, cols], (((1,), (1,)), ((), ())),
                                        preferred_element_type=F32)
                    s = jnp.where(mask, s, NEG_INF)
                    sink = sinks_ref[head0 + par]
                    m = jnp.maximum(jnp.max(s, axis=1, keepdims=True), sink)
                    p = jnp.exp(s - m)
                    denom = jnp.sum(p, axis=1, keepdims=True) + jnp.exp(sink - m)
                    pv = _dot(p.astype(BF16), v_scr[win, cols]) / denom
                    acc = pv if acc is None else acc + pv
                ao_scr[rows, qcols] = acc.astype(BF16)

    o_ref[...] = x + _dot(ao_scr[...], wo_ref[...])


def _attention_layer(x, layer, j, g, w_qkv, q_gain, k_gain, sinks, w_o, *, tq):
    t, d = x.shape
    hd = q_gain.shape[1]
    nq = sinks.shape[1]
    nkv = (w_qkv.shape[2] // hd - nq) // 2
    grp = nq // nkv
    dq, dk = nq * hd, nkv * hd
    assert 2 * hd == LANES and nkv % 2 == 0 and grp % 2 == 0
    assert t % tq == 0 and tq % ATTN_BLOCK == 0 and nq <= LANES

    def head_sum(ncols):
        c = jnp.arange(ncols)[:, None] // hd
        return (c == jnp.arange(LANES)[None, :]).astype(BF16)

    def head_bcast(ncols):
        hrow = jnp.arange(2 * LANES)[:, None] % LANES
        return (hrow == jnp.arange(ncols)[None, :] // hd).astype(BF16)

    gq = (jnp.tile(q_gain[j], nq) * (hd ** -0.5)).reshape(1, dq)
    gk = jnp.tile(k_gain[j], nkv).reshape(1, dk)
    kv_rows = ATTN_BLOCK + tq
    return pl.pallas_call(
        functools.partial(_attn_body, tq=tq, nq=nq, nkv=nkv, hd=hd),
        grid=(t // tq,),
        in_specs=[
            pl.BlockSpec(memory_space=pltpu.SMEM),
            pl.BlockSpec((tq, d), lambda i: (i, 0)),
            _layer_spec(_rows(g), layer),
            _layer_spec(w_qkv, j),
            _layer_spec(w_o, j),
            _const_spec((1, dq)),
            _const_spec((1, dk)),
            _const_spec((dq, LANES)),
            _const_spec((2 * LANES, dq)),
            _const_spec((dk, LANES)),
            _const_spec((2 * LANES, dk)),
        ],
        out_specs=pl.BlockSpec((tq, d), lambda i: (i, 0)),
        out_shape=jax.ShapeDtypeStruct((t, d), F32),
        scratch_shapes=[
            pltpu.VMEM((tq, dq), BF16),
            pltpu.VMEM((kv_rows, nkv * LANES), BF16),
            pltpu.VMEM((kv_rows, nkv * LANES), BF16),
            pltpu.VMEM((kv_rows, nkv * LANES), BF16),
            pltpu.VMEM((kv_rows, nkv * LANES), BF16),
            pltpu.VMEM((tq, dq), BF16),
        ],
        compiler_params=pltpu.CompilerParams(
            dimension_semantics=("arbitrary",),
            vmem_limit_bytes=VMEM_LIMIT_BYTES),
        name="swa_layer",
    )(sinks[j], x, _rows(g), w_qkv, w_o, gq, gk,
      head_sum(dq), head_bcast(dq), head_sum(dk), head_bcast(dk))


def _rec_body(xc_ref, xp_ref, g_ref, win_ref, cw_ref, cb_ref, wa_ref, ba_ref, wi_ref, bi_ref,
              lam_ref, wout_ref, o_ref, halo_scr, carry_scr, y_scr, yprev_scr, h_scr, xy_scr,
              *, tm, nblk, nt):
    i = pl.program_id(0)
    w = y_scr.shape[1]
    bd = w // nblk
    od = o_ref.shape[1] // nblk

    @pl.when(i == 0)
    def _():
        halo_scr[...] = jnp.zeros(halo_scr.shape, F32)
        carry_scr[...] = jnp.zeros(carry_scr.shape, F32)
        y_scr[...] = jnp.zeros(y_scr.shape, BF16)

    def out_project(y_ref, cols):
        o_ref[:, cols] = xp_ref[:, cols] + _dot(y_ref[...], wout_ref[:, cols])

    def mix():
        yprev_scr[...] = y_scr[...]
        h_scr[...] = _rmsnorm(xc_ref[...], g_ref[...]).astype(BF16)
        neg_lam = -lam_ref[...]
        softplus = jnp.maximum(neg_lam, 0.0) + jnp.log1p(jnp.exp(-jnp.abs(neg_lam)))
        row = lax.broadcasted_iota(jnp.int32, (SUBLANES, bd), 0)
        pack = 2 * SUBLANES

        def in_project(b):
            h = h_scr[...]
            xy_scr[b % 2, 0] = _dot(h, win_ref[:, b * bd:(b + 1) * bd])
            xy_scr[b % 2, 1] = _dot(h, win_ref[:, w + b * bd:w + (b + 1) * bd])

        in_project(0)
        for b in range(nblk):
            sl = slice(b * bd, (b + 1) * bd)
            if b + 1 < nblk:
                in_project(b + 1)
            xb = xy_scr[b % 2, 0]
            gate = _gelu_tanh(xy_scr[b % 2, 1])
            xc = _causal_taps_ref(halo_scr, sl, xb, cw_ref[:, sl], cb_ref[:, sl])
            xcb = xc.astype(BF16)
            r = jax.nn.sigmoid(_dot(xcb, wa_ref[b]) + ba_ref[:, sl])
            gi = jax.nn.sigmoid(_dot(xcb, wi_ref[b]) + bi_ref[:, sl])
            log_a = (-LRU_C) * r * softplus[:, sl]
            a = jnp.exp(log_a)
            u = jnp.sqrt(_one_minus_exp2(log_a)) * (gi * xc)
            carry = carry_scr[0:1, sl]
            for p0 in range(0, tm, pack):
                hs = []
                for r0 in range(p0, p0 + pack, SUBLANES):
                    ag, ug = a[r0:r0 + SUBLANES, :], u[r0:r0 + SUBLANES, :]
                    for s in (1, 2, 4):
                        keep = row >= s
                        ug = jnp.where(keep, ag * pltpu.roll(ug, s, 0) + ug, ug)
                        ag = jnp.where(keep, ag * pltpu.roll(ag, s, 0), ag)
                    hg = ag * carry + ug
                    hs.append(hg)
                    carry = hg[SUBLANES - 1:SUBLANES, :]
                y_scr[p0:p0 + pack, sl] = (jnp.concatenate(hs, axis=0)
                                           * gate[p0:p0 + pack, :]).astype(BF16)
            carry_scr[0:1, sl] = carry
            out_project(yprev_scr, slice(b * od, (b + 1) * od))

    @pl.when(i < nt)
    def _():
        mix()

    @pl.when(i == nt)
    def _():
        out_project(y_scr, slice(0, nblk * od))


def _recurrent_layer(x, layer, j, g, w_in, conv_w, conv_b, w_a, b_a, w_i, b_i, lam, w_out, *, tm):
    t, d = x.shape
    w = w_out.shape[1]
    nblk = w_a.shape[1]
    assert t % tm == 0 and tm % (2 * SUBLANES) == 0
    assert (w // nblk) % LANES == 0 and d % (nblk * LANES) == 0
    nt = t // tm
    g, conv_b, lam = _rows(g), _rows(conv_b), _rows(lam)
    b_a = b_a.reshape(b_a.shape[0], 1, w)
    b_i = b_i.reshape(b_i.shape[0], 1, w)
    return pl.pallas_call(
        functools.partial(_rec_body, tm=tm, nblk=nblk, nt=nt),
        grid=(nt + 1,),
        in_specs=[
            pl.BlockSpec((tm, d), lambda i: (jnp.minimum(i, nt - 1), 0)),
            pl.BlockSpec((tm, d), lambda i: (jnp.maximum(i - 1, 0), 0)),
            _layer_spec(g, layer),
            _layer_spec(w_in, j),
            _layer_spec(conv_w, j),
            _layer_spec(conv_b, j),
            _layer_spec(w_a, j),
            _layer_spec(b_a, j),
            _layer_spec(w_i, j),
            _layer_spec(b_i, j),
            _layer_spec(lam, j),
            _layer_spec(w_out, j),
        ],
        out_specs=pl.BlockSpec((tm, d), lambda i: (jnp.maximum(i - 1, 0), 0)),
        out_shape=jax.ShapeDtypeStruct((t, d), F32),
        scratch_shapes=[
            pltpu.VMEM((SUBLANES + tm, w), F32),
            pltpu.VMEM((SUBLANES, w), F32),
            pltpu.VMEM((tm, w), BF16),
            pltpu.VMEM((tm, w), BF16),
            pltpu.VMEM((tm, d), BF16),
            pltpu.VMEM((2, 2, tm, w // nblk), F32),
        ],
        compiler_params=pltpu.CompilerParams(
            dimension_semantics=("arbitrary",),
            vmem_limit_bytes=VMEM_LIMIT_BYTES),
        name="rglru_layer",
    )(x, x, g, w_in, conv_w, conv_b, w_a, b_a, w_i, b_i, lam, w_out)


ATTN_TQ = 256
REC_TM = 256
FFN_TM = 1024
FFN_TF = 512


def kernel(x, mix_norm, ffn_norm, attn_w_qkv, attn_q_gain, attn_k_gain, attn_sinks, attn_w_o,
           rec_w_in, rec_conv_w, rec_conv_b, rec_w_a, rec_b_a, rec_w_i, rec_b_i, rec_lambda,
           rec_w_out, ffn_w_up, ffn_conv_w, ffn_conv_b, ffn_w_down):
    b, t, d = x.shape
    depth = mix_norm.shape[0]
    n_mixers = 2
    attn_w_qkv, attn_w_o, rec_w_in, rec_w_a, rec_w_i, rec_w_out = (
        a.astype(BF16) for a in (attn_w_qkv, attn_w_o, rec_w_in, rec_w_a, rec_w_i, rec_w_out))
    outs = []
    for bi in range(b):
        xs = x[bi]
        w_up, w_down = ffn_w_up[0].astype(BF16), ffn_w_down[0].astype(BF16)
        for layer in range(depth):
            j = layer // n_mixers
            if layer % n_mixers == 0:
                xs = _attention_layer(xs, layer, j, mix_norm, attn_w_qkv, attn_q_gain, attn_k_gain,
                                      attn_sinks, attn_w_o, tq=min(ATTN_TQ, t))
            else:
                xs = _recurrent_layer(xs, layer, j, mix_norm, rec_w_in, rec_conv_w, rec_conv_b,
                                      rec_w_a, rec_b_a, rec_w_i, rec_b_i, rec_lambda, rec_w_out,
                                      tm=min(REC_TM, t))
            nxt = layer + 1
            convert = ((ffn_w_up, nxt), (ffn_w_down, nxt)) if nxt < depth else ()
            xs, converted = _conv_ffn(xs, layer, ffn_norm, w_up, ffn_conv_w, ffn_conv_b, w_down,
                                      convert, tm=min(FFN_TM, t),
                                      tf=min(FFN_TF, ffn_w_down.shape[1]))
            if converted:
                w_up, w_down = converted
        outs.append(xs)
    return jnp.stack(outs, axis=0)
```
